```python
import math, functools
import jax, jax.numpy as jnp
from jax import lax
import numpy as np

D_MODEL = 1024
BATCH = 8
SEQ = 8192
DEPTH = 2
DEC_BATCH = 16
DEC_SEQ = 64
PAST_LEN = 1024

CHUNK = 64
LEFT_CHUNKS = 8
BAND = (LEFT_CHUNKS + 1) * CHUNK
LEFT_ROWS = LEFT_CHUNKS * CHUNK
MIX_W = D_MODEL
ATT_W = MIX_W // 2
SSM_W = MIX_W - ATT_W
HEAD_DIM = 64
N_ATT_HEADS = ATT_W // HEAD_DIM
REL_CLIP = 128
SSM_GROUP_CH = 16
N_SSM_GROUPS = SSM_W // SSM_GROUP_CH
SSM_STATE = 64
IN_W = 3 * ATT_W + SSM_W
N_EXPERTS = 64
N_EXPERT_GROUPS = 8
TOPK_GROUPS = 4
TOP_K = 8
D_EXPERT = D_MODEL // 4
D_SHARED = D_EXPERT
ROUTED_SCALE = 2.5
EXPERT_BLOCK = 256
ALPHA = (2 * DEPTH) ** 0.25
BETA = (8 * DEPTH) ** -0.25

kernel_name = 'hybrid_s5_chunkattn_moe_stream_step'


def _layer_norm(x, g, b, eps=1e-5):
    xf = x.astype(jnp.float32)
    mu = xf.mean(-1, keepdims=True)
    var = jnp.square(xf - mu).mean(-1, keepdims=True)
    return ((xf - mu) * lax.rsqrt(var + eps) * g + b).astype(x.dtype)


def _rms_norm(x, g, eps=1e-6):
    xf = x.astype(jnp.float32)
    return (xf * lax.rsqrt(jnp.square(xf).mean(-1, keepdims=True) + eps) * g).astype(x.dtype)


def _adaln(c, w_ada, b_ada):
    m = jax.nn.silu(c) @ w_ada + b_ada
    return jnp.split(m[:, None, :], 6, axis=-1)


def _band_bias(qpos, kpos, rel_bias):
    rel = qpos - kpos
    idx = jnp.clip(rel, -REL_CLIP, REL_CLIP) + REL_CLIP
    bias = jnp.moveaxis(jnp.take(rel_bias, idx, axis=1), 0, -3).astype(jnp.float32)
    qc = qpos // CHUNK
    kc = kpos // CHUNK
    valid = (kpos >= 0) & (kc <= qc) & (kc >= qc - LEFT_CHUNKS)
    return bias, valid


def _chunk_band_attention(q, k, v, rel_bias):
    B, L, H, Dh = q.shape
    nc = L // CHUNK
    qc = q.reshape(B, nc, CHUNK, H, Dh)
    pad = ((0, 0), (LEFT_CHUNKS, 0), (0, 0), (0, 0), (0, 0))
    kp = jnp.pad(k.reshape(B, nc, CHUNK, H, Dh), pad)
    vp = jnp.pad(v.reshape(B, nc, CHUNK, H, Dh), pad)
    kb = jnp.concatenate([kp[:, j:j + nc] for j in range(LEFT_CHUNKS + 1)], axis=2)
    vb = jnp.concatenate([vp[:, j:j + nc] for j in range(LEFT_CHUNKS + 1)], axis=2)
    ci = jnp.arange(nc)
    qpos = ci[:, None, None] * CHUNK + jnp.arange(CHUNK)[None, :, None]
    kpos = (ci[:, None, None] - LEFT_CHUNKS) * CHUNK + jnp.arange(BAND)[None, None, :]
    bias, valid = _band_bias(qpos, kpos, rel_bias)
    s = jnp.einsum('bcqhd,bckhd->bchqk', qc, kb).astype(jnp.float32) * Dh ** -0.5 + bias
    s = jnp.where(valid[..., None, :, :], s, jnp.finfo(jnp.float32).min)
    p = jax.nn.softmax(s, axis=-1).astype(v.dtype)
    o = jnp.einsum('bchqk,bckhd->bcqhd', p, vb)
    return o.reshape(B, L, H * Dh)


def _cached_band_attention(q, k, v, rel_bias, k_cache, v_cache):
    B, S, H, Dh = q.shape
    R = k_cache.shape[1]
    ka = jnp.concatenate([k_cache.astype(k.dtype), k], axis=1)
    va = jnp.concatenate([v_cache.astype(v.dtype), v], axis=1)
    qpos = PAST_LEN + jnp.arange(S)
    kpos = jnp.concatenate([PAST_LEN - R + jnp.arange(R), qpos])
    bias, valid = _band_bias(qpos[:, None], kpos[None, :], rel_bias)
    s = jnp.einsum('bqhd,bkhd->bhqk', q, ka).astype(jnp.float32) * Dh ** -0.5 + bias
    s = jnp.where(valid[..., None, :, :], s, jnp.finfo(jnp.float32).min)
    p = jax.nn.softmax(s, axis=-1).astype(v.dtype)
    o = jnp.einsum('bhqk,bkhd->bqhd', p, va)
    return o.reshape(B, S, H * Dh)


def _complex_affine_combine(e1, e2):
    ar1, ai1, br1, bi1 = e1
    ar2, ai2, br2, bi2 = e2
    return (ar2 * ar1 - ai2 * ai1,
            ar2 * ai1 + ai2 * ar1,
            ar2 * br1 - ai2 * bi1 + br2,
            ar2 * bi1 + ai2 * br1 + bi2)


def _s5_ssm(u, h0_re, h0_im, lam_re, lam_im, log_step, b_re, b_im, c_re, c_im, d_skip):
    f32 = jnp.float32
    u = u.astype(f32)
    lam_re = lam_re.astype(f32)
    lam_im = lam_im.astype(f32)
    b_re, b_im, c_re, c_im = b_re.astype(f32), b_im.astype(f32), c_re.astype(f32), c_im.astype(f32)
    step = jnp.exp(log_step.astype(f32))[:, None]
    mag = jnp.exp(lam_re * step)
    a_re = mag * jnp.cos(lam_im * step)
    a_im = mag * jnp.sin(lam_im * step)
    den = lam_re * lam_re + lam_im * lam_im
    f_re = ((a_re - 1.0) * lam_re + a_im * lam_im) / den
    f_im = (a_im * lam_re - (a_re - 1.0) * lam_im) / den
    bb_re = f_re[..., None] * b_re - f_im[..., None] * b_im
    bb_im = f_re[..., None] * b_im + f_im[..., None] * b_re
    bu_re = jnp.einsum('blgh,gph->blgp', u, bb_re)
    bu_im = jnp.einsum('blgh,gph->blgp', u, bb_im)
    h0_re = h0_re.astype(f32)
    h0_im = h0_im.astype(f32)
    bu_re = bu_re.at[:, 0].add(a_re * h0_re - a_im * h0_im)
    bu_im = bu_im.at[:, 0].add(a_re * h0_im + a_im * h0_re)
    L = u.shape[1]
    ar = jnp.broadcast_to(a_re, (1, L) + a_re.shape)
    ai = jnp.broadcast_to(a_im, (1, L) + a_im.shape)
    _, _, s_re, s_im = lax.associative_scan(_complex_affine_combine, (ar, ai, bu_re, bu_im), axis=1)
    y = (jnp.einsum('blgp,ghp->blgh', s_re, c_re) - jnp.einsum('blgp,ghp->blgh', s_im, c_im)
         + d_skip.astype(f32) * u)
    return y, s_re[:, -1], s_im[:, -1]


def _route(x, w_router, e_bias):
    f32 = jnp.float32
    T = x.shape[0]
    scores = jax.nn.sigmoid((x @ w_router).astype(f32))
    sel = scores + e_bias.astype(f32)
    gsc = lax.top_k(sel.reshape(T, N_EXPERT_GROUPS, N_EXPERTS // N_EXPERT_GROUPS), 2)[0].sum(-1)
    _, gidx = lax.top_k(gsc, TOPK_GROUPS)
    gmask = jax.nn.one_hot(gidx, N_EXPERT_GROUPS, dtype=f32).sum(1) > 0
    emask = jnp.repeat(gmask, N_EXPERTS // N_EXPERT_GROUPS, axis=1)
    _, eidx = lax.top_k(jnp.where(emask, sel, -jnp.inf), TOP_K)
    w = jnp.take_along_axis(scores, eidx, axis=1)
    w = w / w.sum(-1, keepdims=True) * ROUTED_SCALE
    return eidx, w


def _routed_experts(x, eidx, gates, w_gate, w_up, w_down):
    T, D = x.shape
    A = T * TOP_K
    flat_e = eidx.reshape(A)
    order = jnp.argsort(flat_e).astype(jnp.int32)
    e_sorted = flat_e[order]
    counts = jnp.bincount(flat_e, length=N_EXPERTS)
    padded = (counts + EXPERT_BLOCK - 1) // EXPERT_BLOCK * EXPERT_BLOCK
    pad_end = jnp.cumsum(padded)
    pad_start = pad_end - padded
    start = jnp.cumsum(counts) - counts
    dest = pad_start[e_sorted] + jnp.arange(A) - start[e_sorted]
    n_blocks = -(-A // EXPERT_BLOCK) + N_EXPERTS
    row_asg = jnp.full((n_blocks * EXPERT_BLOCK,), A, jnp.int32).at[dest].set(order)
    block_expert = jnp.minimum(
        jnp.searchsorted(pad_end, jnp.arange(n_blocks) * EXPERT_BLOCK, side='right'), N_EXPERTS - 1)
    tok_of = jnp.concatenate([jnp.arange(A, dtype=jnp.int32) // TOP_K, jnp.array([T], jnp.int32)])
    gate_of = jnp.concatenate([gates.reshape(A), jnp.zeros((1,), gates.dtype)])
    x_ext = jnp.concatenate([x, jnp.zeros((1, D), x.dtype)])

    def body(acc, blk):
        asg, e = blk
        tok = tok_of[asg]
        xb = x_ext[tok]
        h = jax.nn.silu(xb @ w_gate[e]) * (xb @ w_up[e])
        yb = (h @ w_down[e]) * gate_of[asg][:, None].astype(x.dtype)
        return acc.at[tok].add(yb), None

    acc, _ = lax.scan(body, jnp.zeros((T + 1, D), x.dtype),
                      (row_asg.reshape(n_blocks, EXPERT_BLOCK), block_expert))
    return acc[:T]


def _swiglu(x, wg, wu, wd):
    return (jax.nn.silu(x @ wg) * (x @ wu)) @ wd


def _hybrid_layer(x, c, attend, h0_re, h0_im, w_ada, b_ada, w_in, rel_bias, lam_re, lam_im, log_step,
                  b_re, b_im, c_re, c_im, d_skip, w_glu, g_attn, g_ssm, w_out, ln1_g, ln1_b,
                  w_router, e_bias, w_gate, w_up, w_down, sw_gate, sw_up, sw_down, ln2_g, ln2_b):
    B, L, D = x.shape
    sh1, sc1, gt1, sh2, sc2, gt2 = _adaln(c, w_ada, b_ada)
    h = x * (1.0 + sc1) + sh1
    q, k, v, u = jnp.split(h @ w_in, [ATT_W, 2 * ATT_W, 3 * ATT_W], axis=-1)
    heads = (B, L, N_ATT_HEADS, HEAD_DIM)
    q, k, v = q.reshape(heads), k.reshape(heads), v.reshape(heads)
    y_att = attend(q, k, v, rel_bias)
    y_ssm, s_re, s_im = _s5_ssm(u.reshape(B, L, N_SSM_GROUPS, SSM_GROUP_CH), h0_re, h0_im,
                                lam_re, lam_im, log_step, b_re, b_im, c_re, c_im, d_skip)
    y_ssm = jax.nn.gelu(y_ssm.reshape(B, L, SSM_W).astype(x.dtype))
    y_ssm = y_ssm * jax.nn.sigmoid(y_ssm @ w_glu)
    mix = jnp.concatenate([_rms_norm(y_att, g_attn), _rms_norm(y_ssm, g_ssm)], axis=-1) @ w_out
    x = _layer_norm(ALPHA * x + (1.0 + gt1) * mix, ln1_g, ln1_b)
    h = (x * (1.0 + sc2) + sh2).reshape(B * L, D)
    eidx, gates = _route(h, w_router, e_bias)
    ff = _routed_experts(h, eidx, gates, w_gate, w_up, w_down) + _swiglu(h, sw_gate, sw_up, sw_down)
    x = _layer_norm(ALPHA * x + (1.0 + gt2) * ff.reshape(B, L, D), ln2_g, ln2_b)
    return x, k, v, s_re, s_im


def setup_inputs(seed: int = 0) -> dict:
    key = jax.random.key(seed)
    keys = iter(jax.random.split(key, 64))
    f32 = jnp.float32

    def nrm(shape, scale):
        return scale * jax.random.normal(next(keys), shape, f32)

    G, P, H, E, F, d = N_SSM_GROUPS, SSM_STATE, SSM_GROUP_CH, N_EXPERTS, D_EXPERT, D_MODEL
    rows = min(LEFT_ROWS, PAST_LEN)
    return {
        'x_prompt': nrm((BATCH, SEQ, d), 1.0),
        'x_sample': nrm((DEC_BATCH, DEC_SEQ, d), 1.0),
        'cache_k': nrm((DEPTH, DEC_BATCH, rows, N_ATT_HEADS, HEAD_DIM), 1.0),
        'cache_v': nrm((DEPTH, DEC_BATCH, rows, N_ATT_HEADS, HEAD_DIM), 1.0),
        'state_ssm_re': nrm((DEPTH, DEC_BATCH, G, P), 0.5),
        'state_ssm_im': nrm((DEPTH, DEC_BATCH, G, P), 0.5),
        'c_prompt': nrm((BATCH, d), 1.0),
        'c_sample': nrm((DEC_BATCH, d), 1.0),
        'w_ada': nrm((DEPTH, d, 6 * d), 0.5 * d ** -0.5),
        'b_ada': nrm((DEPTH, 6 * d), 0.02),
        'w_in': nrm((DEPTH, d, IN_W), d ** -0.5),
        'rel_bias': nrm((DEPTH, N_ATT_HEADS, 2 * REL_CLIP + 1), 0.5),
        'lam_re': -0.5 + nrm((DEPTH, G, P), 0.01),
        'lam_im': math.pi * jnp.arange(P, dtype=f32) * (1.0 + nrm((DEPTH, G, P), 0.01)),
        'log_step': jax.random.uniform(next(keys), (DEPTH, G), f32, math.log(1e-3), math.log(1e-1)),
        'b_re': nrm((DEPTH, G, P, H), (2 * H) ** -0.5),
        'b_im': nrm((DEPTH, G, P, H), (2 * H) ** -0.5),
        'c_re': nrm((DEPTH, G, H, P), P ** -0.5),
        'c_im': nrm((DEPTH, G, H, P), P ** -0.5),
        'd_skip': nrm((DEPTH, G, H), 0.5),
        'w_glu': nrm((DEPTH, SSM_W, SSM_W), SSM_W ** -0.5),
        'g_attn': 1.0 + nrm((DEPTH, ATT_W), 0.1),
        'g_ssm': 1.0 + nrm((DEPTH, SSM_W), 0.1),
        'w_out': nrm((DEPTH, MIX_W, d), BETA * MIX_W ** -0.5),
        'ln1_g': 1.0 + nrm((DEPTH, d), 0.1),
        'ln1_b': nrm((DEPTH, d), 0.02),
        'w_router': nrm((DEPTH, d, E), d ** -0.5),
        'e_bias': nrm((DEPTH, E), 0.01),
        'w_gate': nrm((DEPTH, E, d, F), d ** -0.5),
        'w_up': nrm((DEPTH, E, d, F), d ** -0.5),
        'w_down': nrm((DEPTH, E, F, d), BETA * F ** -0.5),
        'sw_gate': nrm((DEPTH, d, D_SHARED), d ** -0.5),
        'sw_up': nrm((DEPTH, d, D_SHARED), d ** -0.5),
        'sw_down': nrm((DEPTH, D_SHARED, d), BETA * D_SHARED ** -0.5),
        'ln2_g': 1.0 + nrm((DEPTH, d), 0.1),
        'ln2_b': nrm((DEPTH, d), 0.02),
    }


def reference(x_prompt, x_sample, cache_k, cache_v, state_ssm_re, state_ssm_im, c_prompt, c_sample,
              w_ada, b_ada, w_in, rel_bias, lam_re, lam_im, log_step, b_re, b_im, c_re, c_im, d_skip,
              w_glu, g_attn, g_ssm, w_out, ln1_g, ln1_b, w_router, e_bias, w_gate, w_up, w_down,
              sw_gate, sw_up, sw_down, ln2_g, ln2_b):
    yp, ys = x_prompt, x_sample
    kp_l, vp_l, rp_l, ip_l = [], [], [], []
    ks_l, vs_l, rs_l, is_l = [], [], [], []
    h0p = jnp.zeros((x_prompt.shape[0], N_SSM_GROUPS, SSM_STATE), jnp.float32)
    keep = min(LEFT_ROWS, x_prompt.shape[1])
    for l in range(DEPTH):
        lw = (w_ada[l], b_ada[l], w_in[l], rel_bias[l], lam_re[l], lam_im[l], log_step[l],
              b_re[l], b_im[l], c_re[l], c_im[l], d_skip[l], w_glu[l], g_attn[l], g_ssm[l], w_out[l],
              ln1_g[l], ln1_b[l], w_router[l], e_bias[l], w_gate[l], w_up[l], w_down[l],
              sw_gate[l], sw_up[l], sw_down[l], ln2_g[l], ln2_b[l])
        yp, k, v, sr, si = _hybrid_layer(yp, c_prompt, _chunk_band_attention, h0p, h0p, *lw)
        kp_l.append(k[:, k.shape[1] - keep:])
        vp_l.append(v[:, v.shape[1] - keep:])
        rp_l.append(sr)
        ip_l.append(si)
        attend_s = functools.partial(_cached_band_attention, k_cache=cache_k[l], v_cache=cache_v[l])
        ys, k, v, sr, si = _hybrid_layer(ys, c_sample, attend_s, state_ssm_re[l], state_ssm_im[l], *lw)
        ks_l.append(k)
        vs_l.append(v)
        rs_l.append(sr)
        is_l.append(si)
    return (yp, ys, jnp.stack(kp_l), jnp.stack(vp_l), jnp.stack(rp_l), jnp.stack(ip_l),
            jnp.stack(ks_l), jnp.stack(vs_l), jnp.stack(rs_l), jnp.stack(is_l))
```

```python
import functools
import math

import jax
import jax.numpy as jnp
from jax import lax
from jax.experimental import pallas as pl
from jax.experimental.pallas import tpu as pltpu

F32 = jnp.float32
BF16 = jnp.bfloat16
I32 = jnp.int32

D_MODEL = 1024
DEPTH = 2
CHUNK = 64
LEFT_CHUNKS = 8
BAND = (LEFT_CHUNKS + 1) * CHUNK
LEFT_ROWS = LEFT_CHUNKS * CHUNK
ATT_W = 512
SSM_W = 512
HEAD_DIM = 64
N_HEADS = ATT_W // HEAD_DIM
N_PAIRS = N_HEADS // 2
REL_CLIP = 128
SSM_GROUP_CH = 16
N_SSM_GROUPS = SSM_W // SSM_GROUP_CH
SSM_STATE = 64
SSM_CH = N_SSM_GROUPS * SSM_STATE
IN_W = 3 * ATT_W + SSM_W
N_EXPERTS = 64
N_EXPERT_GROUPS = 8
GROUP_SIZE = N_EXPERTS // N_EXPERT_GROUPS
TOPK_GROUPS = 4
TOP_K = 8
D_EXPERT = D_MODEL // 4
ROUTED_SCALE = 2.5
EXPERT_BLOCK = 256
ALPHA = (2 * DEPTH) ** 0.25
MASK_VALUE = -1e30
LANES = 128
VMEM_LIMIT = 52 * 1024 * 1024


def _dot(a, b):
    return jnp.dot(a, b, preferred_element_type=F32)


def _split_bf16(x):
    hi = x.astype(BF16)
    lo = (x - hi.astype(F32)).astype(BF16)
    return hi, lo


def _silu(x):
    return x * jax.nn.sigmoid(x)


def _layer_norm(x, g, b, eps=1e-5):
    mu = jnp.mean(x, axis=-1, keepdims=True)
    xc = x - mu
    var = jnp.mean(xc * xc, axis=-1, keepdims=True)
    return xc * lax.rsqrt(var + eps) * g + b


def _rms_norm(x, g, eps=1e-6):
    return x * lax.rsqrt(jnp.mean(x * x, axis=-1, keepdims=True) + eps) * g


def _params(sem, vmem=VMEM_LIMIT):
    return pltpu.CompilerParams(dimension_semantics=sem, vmem_limit_bytes=vmem)


def _ada_kernel(c_ref, w_ref, b_ref, o_ref):
    s_hi, s_lo = _split_bf16(_silu(c_ref[...]))
    w_hi, w_lo = _split_bf16(w_ref[0])
    o_ref[0] = _dot(s_hi, w_hi) + _dot(s_hi, w_lo) + _dot(s_lo, w_hi) + b_ref[0]


def _ada_call(c_all, w_ada, b_ada):
    nb, d = c_all.shape
    depth, _, n = w_ada.shape
    tn = 1536
    return pl.pallas_call(
        _ada_kernel,
        grid=(depth, n // tn),
        in_specs=[pl.BlockSpec((nb, d), lambda l, j: (0, 0)),
                  pl.BlockSpec((1, d, tn), lambda l, j: (l, 0, j)),
                  pl.BlockSpec((1, 1, tn), lambda l, j: (l, 0, j))],
        out_specs=pl.BlockSpec((1, nb, tn), lambda l, j: (l, 0, j)),
        out_shape=jax.ShapeDtypeStruct((depth, nb, n), F32),
        compiler_params=_params(("arbitrary", "arbitrary")),
        name="ada",
    )(c_all, w_ada, b_ada.reshape(depth, 1, n))


def _inproj_kernel(x_ref, sc_ref, sh_ref, w_ref, q_ref, k_ref, v_ref, u_ref, kf_ref, vf_ref, *, first_keep):
    i = pl.program_id(1)
    h = (x_ref[0] * (1.0 + sc_ref[0]) + sh_ref[0]).astype(BF16)
    q = _dot(h, w_ref[:, 0:ATT_W])
    q_ref[0] = (q * HEAD_DIM ** -0.5).astype(BF16)
    k = _dot(h, w_ref[:, ATT_W:2 * ATT_W])
    k_ref[0] = k.astype(BF16)
    v = _dot(h, w_ref[:, 2 * ATT_W:3 * ATT_W])
    v_ref[0] = v.astype(BF16)
    u_ref[...] = _dot(h, w_ref[:, 3 * ATT_W:IN_W])

    @pl.when(i >= first_keep)
    def _():
        kf_ref[0] = k
        vf_ref[0] = v


def _inproj_call(x, sc, sh, w_in_bf, tl, keep):
    b, l, d = x.shape
    nl = l // tl
    first_keep = nl - keep // tl
    row = lambda bb, i: (bb, i, 0)
    vec = lambda bb, i: (bb, 0, 0)
    kept = lambda bb, i: (bb, jnp.maximum(i - first_keep, 0), 0)
    return pl.pallas_call(
        functools.partial(_inproj_kernel, first_keep=first_keep),
        grid=(b, nl),
        in_specs=[pl.BlockSpec((1, tl, d), row),
                  pl.BlockSpec((1, 1, d), vec),
                  pl.BlockSpec((1, 1, d), vec),
                  pl.BlockSpec((d, IN_W), lambda bb, i: (0, 0))],
        out_specs=[pl.BlockSpec((1, tl, ATT_W), row),
                   pl.BlockSpec((1, tl, ATT_W), row),
                   pl.BlockSpec((1, tl, ATT_W), row),
                   pl.BlockSpec((tl, SSM_W), lambda bb, i: (i, bb)),
                   pl.BlockSpec((1, tl, ATT_W), kept),
                   pl.BlockSpec((1, tl, ATT_W), kept)],
        out_shape=[jax.ShapeDtypeStruct((b, l, ATT_W), BF16),
                   jax.ShapeDtypeStruct((b, l, ATT_W), BF16),
                   jax.ShapeDtypeStruct((b, l, ATT_W), BF16),
                   jax.ShapeDtypeStruct((l, b * SSM_W), F32),
                   jax.ShapeDtypeStruct((b, keep, ATT_W), F32),
                   jax.ShapeDtypeStruct((b, keep, ATT_W), F32)],
        compiler_params=_params(("arbitrary", "arbitrary")),
        name="inproj",
    )(x, sc, sh, w_in_bf)


def _attn_kernel(q_ref, ka_ref, kb_ref, va_ref, vb_ref, bias_ref, g_ref, o_ref, kw, vw, ob, *, ra, tq, mask_first):
    i = pl.program_id(1)
    kw[0:ra] = ka_ref[0]
    kw[ra:ra + tq] = kb_ref[0]
    vw[0:ra] = va_ref[0]
    vw[ra:ra + tq] = vb_ref[0]
    low = lax.broadcasted_iota(I32, (1, LANES), 1) < HEAD_DIM
    col = lax.broadcasted_iota(I32, (1, BAND), 1)

    def chunk(j, carry):
        r0 = pl.multiple_of(j * CHUNK, CHUNK)
        off = pl.multiple_of(r0 + (ra - LEFT_ROWS), CHUNK)
        qc = q_ref[0, pl.ds(r0, CHUNK), :]
        for p in range(N_PAIRS):
            ls = slice(p * LANES, (p + 1) * LANES)
            qp = qc[:, ls]
            zero = jnp.zeros_like(qp)
            q2 = jnp.concatenate([jnp.where(low, qp, zero), jnp.where(low, zero, qp)], axis=0)
            kp = kw[pl.ds(off, BAND), ls]
            vp = vw[pl.ds(off, BAND), ls]
            s = lax.dot_general(q2, kp, (((1,), (1,)), ((), ())), preferred_element_type=F32)
            s = s + bias_ref[p]
            if mask_first:
                thr = jnp.where(i == 0, LEFT_ROWS - r0, 0)
                s = jnp.where(col >= thr, s, MASK_VALUE)
            m = jnp.max(s, axis=-1, keepdims=True)
            e = jnp.exp(s - m)
            den = jnp.sum(e, axis=-1, keepdims=True)
            pv = _dot(e.astype(BF16), vp) / den
            ob[pl.ds(r0, CHUNK), ls] = jnp.where(low, pv[0:CHUNK], pv[CHUNK:2 * CHUNK])
        return carry

    lax.fori_loop(0, tq // CHUNK, chunk, 0)
    o_ref[0] = _rms_norm(ob[...], g_ref[...]).astype(BF16)


def _attn_call(q, ka, kb, va, vb, bias, g_attn, *, tq, ra, prompt):
    b, l, _ = q.shape
    nt = l // tq
    cur = lambda bb, i: (bb, i, 0)
    if prompt:
        prev = lambda bb, i: (bb, jnp.maximum(i - 1, 0), 0)
    else:
        prev = lambda bb, i: (bb, 0, 0)
    return pl.pallas_call(
        functools.partial(_attn_kernel, ra=ra, tq=tq, mask_first=prompt),
        grid=(b, nt),
        in_specs=[pl.BlockSpec((1, tq, ATT_W), cur),
                  pl.BlockSpec((1, ra, ATT_W), prev),
                  pl.BlockSpec((1, tq, ATT_W), cur),
                  pl.BlockSpec((1, ra, ATT_W), prev),
                  pl.BlockSpec((1, tq, ATT_W), cur),
                  pl.BlockSpec((N_PAIRS, 2 * CHUNK, BAND), lambda bb, i: (0, 0, 0)),
                  pl.BlockSpec((1, ATT_W), lambda bb, i: (0, 0))],
        out_specs=pl.BlockSpec((1, tq, ATT_W), cur),
        out_shape=jax.ShapeDtypeStruct((b, l, ATT_W), BF16),
        scratch_shapes=[pltpu.VMEM((ra + tq, ATT_W), BF16),
                        pltpu.VMEM((ra + tq, ATT_W), BF16),
                        pltpu.VMEM((tq, ATT_W), F32)],
        compiler_params=_params(("arbitrary", "arbitrary")),
        name="attn",
    )(q, ka, kb, va, vb, bias, g_attn)


def _band_bias_table(rel_bias):
    rel = LEFT_ROWS + jnp.arange(CHUNK)[:, None] - jnp.arange(BAND)[None, :]
    idx = jnp.clip(rel, -REL_CLIP, REL_CLIP) + REL_CLIP
    tab = jnp.take(rel_bias.astype(F32), idx, axis=1)
    return tab.reshape(N_PAIRS, 2 * CHUNK, BAND)


def _gelu_tanh(x):
    c = math.sqrt(2.0 / math.pi)
    return x * (0.5 * (1.0 + jnp.tanh(c * (x + 0.044715 * (x * x * x)))))


def _ssm_kernel(u_ref, h0r_ref, h0i_ref, ar_ref, ai_ref, wre_ref, wim_ref, cre_ref, cim_ref, dsk_ref,
                wglu_ref, g_ref, o_ref, sre_ref, sim_ref, bre, bim, st_re, st_im, *, nb, tl):
    i = pl.program_id(0)
    cw = SSM_CH // 4
    uw = SSM_W // 4

    @pl.when(i == 0)
    def _():
        st_re[...] = h0r_ref[...]
        st_im[...] = h0i_ref[...]

    u = u_ref[...]
    ub = u.astype(BF16)
    for d in range(4):
        ud = ub[:, d * uw:(d + 1) * uw]
        bre[:, d * cw:(d + 1) * cw] = _dot(ud, wre_ref[d])
        bim[:, d * cw:(d + 1) * cw] = _dot(ud, wim_ref[d])

    for d in range(4):
        cs = slice(d * cw, (d + 1) * cw)
        a_re = jnp.broadcast_to(ar_ref[:, cs], (nb, cw))
        a_im = jnp.broadcast_to(ai_ref[:, cs], (nb, cw))

        def step(t, carry, cs=cs, a_re=a_re, a_im=a_im):
            s_re, s_im = carry
            rows = pl.ds(pl.multiple_of(t * nb, nb), nb)
            n_re = a_re * s_re - a_im * s_im + bre[rows, cs]
            n_im = a_re * s_im + a_im * s_re + bim[rows, cs]
            bre[rows, cs] = n_re
            bim[rows, cs] = n_im
            return n_re, n_im

        s_re, s_im = lax.fori_loop(0, tl, step, (st_re[:, cs], st_im[:, cs]))
        st_re[:, cs] = s_re
        st_im[:, cs] = s_im

    ys = []
    for d in range(4):
        cs = slice(d * cw, (d + 1) * cw)
        ys.append(_dot(bre[:, cs].astype(BF16), cre_ref[d]) - _dot(bim[:, cs].astype(BF16), cim_ref[d]))
    y = jnp.concatenate(ys, axis=1) + dsk_ref[...] * u
    g = _gelu_tanh(y)
    o = g * jax.nn.sigmoid(_dot(g.astype(BF16), wglu_ref[...]))
    o_ref[...] = _rms_norm(o, g_ref[...]).astype(BF16)
    sre_ref[...] = st_re[...]
    sim_ref[...] = st_im[...]


def _ssm_call(u_tm, h0_re, h0_im, sp, w_glu_bf, g_ssm, *, nb, tl):
    rows_total = u_tm.shape[0]
    r = nb * tl
    steps = rows_total // r
    const2 = lambda i: (0, 0)
    const3 = lambda i: (0, 0, 0)
    uw = SSM_W // 4
    cw = SSM_CH // 4
    return pl.pallas_call(
        functools.partial(_ssm_kernel, nb=nb, tl=tl),
        grid=(steps,),
        in_specs=[pl.BlockSpec((r, SSM_W), lambda i: (i, 0)),
                  pl.BlockSpec((nb, SSM_CH), const2),
                  pl.BlockSpec((nb, SSM_CH), const2),
                  pl.BlockSpec((1, SSM_CH), const2),
                  pl.BlockSpec((1, SSM_CH), const2),
                  pl.BlockSpec((4, uw, cw), const3),
                  pl.BlockSpec((4, uw, cw), const3),
                  pl.BlockSpec((4, cw, uw), const3),
                  pl.BlockSpec((4, cw, uw), const3),
                  pl.BlockSpec((1, SSM_W), const2),
                  pl.BlockSpec((SSM_W, SSM_W), const2),
                  pl.BlockSpec((1, SSM_W), const2)],
        out_specs=[pl.BlockSpec((r, SSM_W), lambda i: (i, 0)),
                   pl.BlockSpec((nb, SSM_CH), const2),
                   pl.BlockSpec((nb, SSM_CH), const2)],
        out_shape=[jax.ShapeDtypeStruct((rows_total, SSM_W), BF16),
                   jax.ShapeDtypeStruct((nb, SSM_CH), F32),
                   jax.ShapeDtypeStruct((nb, SSM_CH), F32)],
        scratch_shapes=[pltpu.VMEM((r, SSM_CH), F32),
                        pltpu.VMEM((r, SSM_CH), F32),
                        pltpu.VMEM((nb, SSM_CH), F32),
                        pltpu.VMEM((nb, SSM_CH), F32)],
        compiler_params=_params(("arbitrary",)),
        name="ssm",
    )(u_tm, h0_re, h0_im, sp["a_re"], sp["a_im"], sp["w_re"], sp["w_im"], sp["c_re"], sp["c_im"],
      sp["d_skip"], w_glu_bf, g_ssm)


def _ssm_params(lam_re, lam_im, log_step, b_re, b_im, c_re, c_im, d_skip):
    step = jnp.exp(log_step)[:, None]
    mag = jnp.exp(lam_re * step)
    a_re = mag * jnp.cos(lam_im * step)
    a_im = mag * jnp.sin(lam_im * step)
    den = lam_re * lam_re + lam_im * lam_im
    f_re = ((a_re - 1.0) * lam_re + a_im * lam_im) / den
    f_im = (a_im * lam_re - (a_re - 1.0) * lam_im) / den
    bb_re = f_re[..., None] * b_re - f_im[..., None] * b_im
    bb_im = f_re[..., None] * b_im + f_im[..., None] * b_re
    gp = N_SSM_GROUPS // 4
    eye = jnp.eye(gp, dtype=F32)

    def pack_in(bb):
        t = bb.transpose(0, 2, 1).reshape(4, gp, SSM_GROUP_CH, SSM_STATE)
        w = jnp.einsum("dghp,gk->dghkp", t, eye)
        return w.reshape(4, gp * SSM_GROUP_CH, gp * SSM_STATE).astype(BF16)

    def pack_out(c):
        t = c.transpose(0, 2, 1).reshape(4, gp, SSM_STATE, SSM_GROUP_CH)
        w = jnp.einsum("dgph,gk->dgpkh", t, eye)
        return w.reshape(4, gp * SSM_STATE, gp * SSM_GROUP_CH).astype(BF16)

    return dict(a_re=a_re.reshape(1, SSM_CH), a_im=a_im.reshape(1, SSM_CH),
                w_re=pack_in(bb_re), w_im=pack_in(bb_im), c_re=pack_out(c_re), c_im=pack_out(c_im),
                d_skip=d_skip.reshape(1, SSM_W))


def _route(scores, sel, tl):
    neg = -jnp.inf
    io_g = lax.broadcasted_iota(I32, (GROUP_SIZE, tl), 0)
    gsc = []
    for g in range(N_EXPERT_GROUPS):
        xg = sel[g * GROUP_SIZE:(g + 1) * GROUP_SIZE]
        m1 = jnp.max(xg, axis=0, keepdims=True)
        i1 = jnp.min(jnp.where(xg == m1, io_g, GROUP_SIZE), axis=0, keepdims=True)
        m2 = jnp.max(jnp.where(io_g == i1, neg, xg), axis=0, keepdims=True)
        gsc.append(m1 + m2)
    masked = []
    for g in range(N_EXPERT_GROUPS):
        ahead = jnp.zeros((1, tl), I32)
        for h in range(N_EXPERT_GROUPS):
            if h == g:
                continue
            beats = (gsc[h] >= gsc[g]) if h < g else (gsc[h] > gsc[g])
            ahead = ahead + beats.astype(I32)
        keep = ahead < TOPK_GROUPS
        masked.append(jnp.where(keep, sel[g * GROUP_SIZE:(g + 1) * GROUP_SIZE], neg))
    cur = jnp.concatenate(masked, axis=0)
    io_e = lax.broadcasted_iota(I32, (N_EXPERTS, tl), 0)
    chosen = jnp.zeros((N_EXPERTS, tl), F32)
    ids, ws = [], []
    for _ in range(TOP_K):
        m = jnp.max(cur, axis=0, keepdims=True)
        ik = jnp.min(jnp.where(cur == m, io_e, N_EXPERTS), axis=0, keepdims=True)
        hit = io_e == ik
        ws.append(jnp.sum(jnp.where(hit, scores, 0.0), axis=0, keepdims=True))
        cur = jnp.where(hit, neg, cur)
        chosen = jnp.where(hit, 1.0, chosen)
        ids.append(ik)
    return ids, ws, chosen, io_e


def _stack_rows(rows, tl, dtype):
    io = lax.broadcasted_iota(I32, (len(rows), tl), 0)
    out = jnp.zeros((len(rows), tl), dtype)
    for k, r in enumerate(rows):
        out = jnp.where(io == k, r, out)
    return out


def _postmix_kernel(att_ref, ssm_ref, x_ref, wo_ref, g1_ref, sc2_ref, sh2_ref, lng_ref, lnb_ref,
                    wrh_ref, wrl_ref, eb_ref, tri_ref, x1_ref, h2_ref, eidx_ref, gate_ref, rank_ref, cnt_ref,
                    base, *, tl):
    first = jnp.logical_and(pl.program_id(0) == 0, pl.program_id(1) == 0)

    @pl.when(first)
    def _():
        base[...] = jnp.zeros_like(base)

    mix = _dot(att_ref[0], wo_ref[0:ATT_W]) + _dot(ssm_ref[...], wo_ref[ATT_W:ATT_W + SSM_W])
    x1 = _layer_norm(ALPHA * x_ref[0] + (1.0 + g1_ref[0]) * mix, lng_ref[...], lnb_ref[...])
    x1_ref[0] = x1
    h2 = x1 * (1.0 + sc2_ref[0]) + sh2_ref[0]
    h2_ref[0] = h2

    h_hi, h_lo = _split_bf16(h2)
    logits = _dot(h_hi, wrh_ref[...]) + _dot(h_lo, wrh_ref[...]) + _dot(h_hi, wrl_ref[...])
    scores = jax.nn.sigmoid(logits.T[0:N_EXPERTS])
    sel = scores + eb_ref[...]
    ids, ws, chosen, io_e = _route(scores, sel, tl)

    wsum = ws[0]
    for w in ws[1:]:
        wsum = wsum + w
    gate_ref[0] = _stack_rows([w / wsum * ROUTED_SCALE for w in ws], tl, F32)
    eidx_ref[0] = _stack_rows(ids, tl, I32)

    cum = _dot(chosen.astype(BF16), tri_ref[...])
    pos = base[...] + cum.astype(I32) - 1
    rank_ref[0] = _stack_rows([jnp.sum(jnp.where(io_e == ik, pos, 0), axis=0, keepdims=True) for ik in ids],
                              tl, I32)
    base[...] = base[...] + jnp.sum(chosen, axis=1, keepdims=True).astype(I32)
    cnt_ref[...] = base[...]


def _postmix_call(att_n, ssm_n, x, w_out_bf, g1, sc2, sh2, ln_g, ln_b, wr_hi, wr_lo, e_bias, tl):
    b, l, d = x.shape
    nl = l // tl
    t = b * l
    row = lambda bb, i: (bb, i, 0)
    vec = lambda bb, i: (bb, 0, 0)
    const2 = lambda bb, i: (0, 0)
    tok = lambda bb, i: (bb * nl + i, 0, 0)
    tri = jnp.triu(jnp.ones((tl, tl), BF16))
    return pl.pallas_call(
        functools.partial(_postmix_kernel, tl=tl),
        grid=(b, nl),
        in_specs=[pl.BlockSpec((1, tl, ATT_W), row),
                  pl.BlockSpec((tl, SSM_W), lambda bb, i: (i, bb)),
                  pl.BlockSpec((1, tl, d), row),
                  pl.BlockSpec((ATT_W + SSM_W, d), const2),
                  pl.BlockSpec((1, 1, d), vec),
                  pl.BlockSpec((1, 1, d), vec),
                  pl.BlockSpec((1, 1, d), vec),
                  pl.BlockSpec((1, d), const2),
                  pl.BlockSpec((1, d), const2),
                  pl.BlockSpec((d, LANES), const2),
                  pl.BlockSpec((d, LANES), const2),
                  pl.BlockSpec((N_EXPERTS, 1), const2),
                  pl.BlockSpec((tl, tl), const2)],
        out_specs=[pl.BlockSpec((1, tl, d), row),
                   pl.BlockSpec((1, tl, d), row),
                   pl.BlockSpec((1, TOP_K, tl), tok),
                   pl.BlockSpec((1, TOP_K, tl), tok),
                   pl.BlockSpec((1, TOP_K, tl), tok),
                   pl.BlockSpec((N_EXPERTS, 1), const2)],
        out_shape=[jax.ShapeDtypeStruct((b, l, d), F32),
                   jax.ShapeDtypeStruct((b, l, d), F32),
                   jax.ShapeDtypeStruct((t // tl, TOP_K, tl), I32),
                   jax.ShapeDtypeStruct((t // tl, TOP_K, tl), F32),
                   jax.ShapeDtypeStruct((t // tl, TOP_K, tl), I32),
                   jax.ShapeDtypeStruct((N_EXPERTS, 1), I32)],
        scratch_shapes=[pltpu.VMEM((N_EXPERTS, 1), I32)],
        compiler_params=_params(("arbitrary", "arbitrary")),
        name="postmix",
    )(att_n, ssm_n, x, w_out_bf, g1, sc2, sh2, ln_g, ln_b, wr_hi, wr_lo, e_bias, tri)


def _plan_kernel(cnt_ref, start_ref, be_ref, nu_ref, *, n_blocks):
    def per_expert(e, blk):
        nb = (cnt_ref[e] + (EXPERT_BLOCK - 1)) // EXPERT_BLOCK
        start_ref[e] = blk * EXPERT_BLOCK

        def mark(j, c):
            be_ref[j] = e
            return c

        lax.fori_loop(blk, blk + nb, mark, 0)
        return blk + nb

    used = lax.fori_loop(0, N_EXPERTS, per_expert, 0)

    def tail(j, c):
        be_ref[j] = N_EXPERTS - 1
        return c

    lax.fori_loop(used, n_blocks, tail, 0)
    nu_ref[0] = used


def _plan_call(counts, n_blocks):
    smem = pl.BlockSpec(memory_space=pltpu.SMEM)
    return pl.pallas_call(
        functools.partial(_plan_kernel, n_blocks=n_blocks),
        in_specs=[smem],
        out_specs=[smem, smem, smem],
        out_shape=[jax.ShapeDtypeStruct((N_EXPERTS,), I32),
                   jax.ShapeDtypeStruct((n_blocks,), I32),
                   jax.ShapeDtypeStruct((1,), I32)],
        name="plan",
    )(counts)


def _scatter_kernel(start_ref, cnt_ref, eidx_ref, rank_ref, h2_hbm, xs_hbm, dest_ref, zrow, sem, *, tm, nt):
    j = pl.program_id(0)
    tok0 = j * tm

    def row_copy(src_row, dst_row):
        return pltpu.make_async_copy(h2_hbm.at[pl.ds(src_row, 1)], xs_hbm.at[pl.ds(dst_row, 1)], sem)

    def issue(r, c):
        for k in range(TOP_K):
            dst = start_ref[eidx_ref[0, 0, k * tm + r]] + rank_ref[0, 0, k * tm + r]
            dest_ref[0, 0, k * tm + r] = dst
            row_copy(tok0 + r, dst).start()
        return c

    lax.fori_loop(0, tm, issue, 0)

    def drain(r, c):
        row_copy(0, 0).wait()
        return c

    lax.fori_loop(0, tm * TOP_K, drain, 0)

    @pl.when(j == nt - 1)
    def _():
        zrow[...] = jnp.zeros_like(zrow)

        def pad_copy(dst_row):
            return pltpu.make_async_copy(zrow, xs_hbm.at[pl.ds(dst_row, 1)], sem)

        def per_expert(e, c):
            n = cnt_ref[e]
            first = start_ref[e] + n
            npad = (n + (EXPERT_BLOCK - 1)) // EXPERT_BLOCK * EXPERT_BLOCK - n

            def start(r, c2):
                pad_copy(first + r).start()
                return c2

            def wait(r, c2):
                pad_copy(0).wait()
                return c2

            lax.fori_loop(0, npad, start, 0)
            lax.fori_loop(0, npad, wait, 0)
            return c

        lax.fori_loop(0, N_EXPERTS, per_expert, 0)


def _scatter_call(starts, counts, eidx_t, rank_t, h2_flat, n_blocks, tm):
    t, d = h2_flat.shape
    nt = t // tm
    smem_whole = pl.BlockSpec(memory_space=pltpu.SMEM)
    tile = pl.BlockSpec((1, 1, TOP_K * tm), lambda j: (j, 0, 0), memory_space=pltpu.SMEM)
    return pl.pallas_call(
        functools.partial(_scatter_kernel, tm=tm, nt=nt),
        grid=(nt,),
        in_specs=[smem_whole, smem_whole, tile, tile, pl.BlockSpec(memory_space=pl.ANY)],
        out_specs=[pl.BlockSpec(memory_space=pl.ANY), tile],
        out_shape=[jax.ShapeDtypeStruct((n_blocks * EXPERT_BLOCK, d), F32),
                   jax.ShapeDtypeStruct((nt, 1, TOP_K * tm), I32)],
        scratch_shapes=[pltpu.VMEM((1, d), F32), pltpu.SemaphoreType.DMA(())],
        compiler_params=pltpu.CompilerParams(dimension_semantics=("arbitrary",), has_side_effects=True),
        name="scatter",
    )(starts, counts, eidx_t, rank_t, h2_flat)


def _moe_kernel(be_ref, nu_ref, xs_ref, wg_ref, wu_ref, wd_ref, ys_ref):
    j = pl.program_id(0)

    @pl.when(j < nu_ref[0])
    def _():
        xb = xs_ref[...].astype(BF16)
        h = _silu(_dot(xb, wg_ref[0])) * _dot(xb, wu_ref[0])
        ys_ref[...] = _dot(h.astype(BF16), wd_ref[0])

    @pl.when(j >= nu_ref[0])
    def _():
        ys_ref[...] = jnp.zeros_like(ys_ref)


def _moe_call(block_expert, n_used, xs, wg_bf, wu_bf, wd_bf):
    rows, d = xs.shape
    n_blocks = rows // EXPERT_BLOCK
    f = wg_bf.shape[-1]
    grid_spec = pltpu.PrefetchScalarGridSpec(
        num_scalar_prefetch=2,
        grid=(n_blocks,),
        in_specs=[pl.BlockSpec((EXPERT_BLOCK, d), lambda j, be, nu: (jnp.minimum(j, nu[0] - 1), 0)),
                  pl.BlockSpec((1, d, f), lambda j, be, nu: (be[j], 0, 0)),
                  pl.BlockSpec((1, d, f), lambda j, be, nu: (be[j], 0, 0)),
                  pl.BlockSpec((1, f, d), lambda j, be, nu: (be[j], 0, 0))],
        out_specs=pl.BlockSpec((EXPERT_BLOCK, d), lambda j, be, nu: (j, 0)),
    )
    return pl.pallas_call(
        _moe_kernel,
        grid_spec=grid_spec,
        out_shape=jax.ShapeDtypeStruct((rows, d), F32),
        compiler_params=_params(("arbitrary",)),
        name="moe",
    )(block_expert, n_used, xs, wg_bf, wu_bf, wd_bf)


def _final_kernel(dest_ref, x1_ref, h2_ref, gt_ref, g2_ref, swg_ref, swu_ref, swd_ref, lng_ref, lnb_ref, ys_hbm,
                  o_ref, buf, sem, *, tm):
    def row_copy(src_row, k, r):
        return pltpu.make_async_copy(ys_hbm.at[pl.ds(src_row, 1)], buf.at[k, pl.ds(r, 1)], sem)

    def issue(r, c):
        for k in range(TOP_K):
            row_copy(dest_ref[0, 0, k * tm + r], k, r).start()
        return c

    lax.fori_loop(0, tm, issue, 0)

    hb = h2_ref[0].astype(BF16)
    ff = _dot((_silu(_dot(hb, swg_ref[...])) * _dot(hb, swu_ref[...])).astype(BF16), swd_ref[...])

    def drain(r, c):
        row_copy(0, 0, 0).wait()
        return c

    lax.fori_loop(0, tm * TOP_K, drain, 0)

    gt = gt_ref[...]
    for k in range(TOP_K):
        ff = ff + gt[:, k:k + 1] * buf[k]
    o_ref[0] = _layer_norm(ALPHA * x1_ref[0] + (1.0 + g2_ref[0]) * ff, lng_ref[...], lnb_ref[...])


def _final_call(dest, x1, h2, gates_tok, g2, swg_bf, swu_bf, swd_bf, ln_g, ln_b, ys, tm):
    b, l, d = x1.shape
    nl = l // tm
    f = swg_bf.shape[-1]
    row = lambda bb, i: (bb, i, 0)
    vec = lambda bb, i: (bb, 0, 0)
    const2 = lambda bb, i: (0, 0)
    return pl.pallas_call(
        functools.partial(_final_kernel, tm=tm),
        grid=(b, nl),
        in_specs=[pl.BlockSpec((1, 1, TOP_K * tm), lambda bb, i: (bb * nl + i, 0, 0), memory_space=pltpu.SMEM),
                  pl.BlockSpec((1, tm, d), row),
                  pl.BlockSpec((1, tm, d), row),
                  pl.BlockSpec((tm, TOP_K), lambda bb, i: (bb * nl + i, 0)),
                  pl.BlockSpec((1, 1, d), vec),
                  pl.BlockSpec((d, f), const2),
                  pl.BlockSpec((d, f), const2),
                  pl.BlockSpec((f, d), const2),
                  pl.BlockSpec((1, d), const2),
                  pl.BlockSpec((1, d), const2),
                  pl.BlockSpec(memory_space=pl.ANY)],
        out_specs=pl.BlockSpec((1, tm, d), row),
        out_shape=jax.ShapeDtypeStruct((b, l, d), F32),
        scratch_shapes=[pltpu.VMEM((TOP_K, tm, d), F32), pltpu.SemaphoreType.DMA(())],
        compiler_params=_params(("arbitrary", "arbitrary")),
        name="final",
    )(dest, x1, h2, gates_tok, g2, swg_bf, swu_bf, swd_bf, ln_g, ln_b, ys)


def _tile_major(a, tm):
    n, k, tl = a.shape
    return a.reshape(n, k, tl // tm, tm).transpose(0, 2, 1, 3).reshape(n * (tl // tm), 1, k * tm)


def _layer(x, mod, attend_kv, h0_re, h0_im, lw, *, prompt):
    b, l, d = x.shape
    sh1, sc1, g1, sh2, sc2, g2 = mod
    tl = min(512, l)
    keep = min(LEFT_ROWS, l)
    q, k_bf, v_bf, u_tm, k_keep, v_keep = _inproj_call(x, sc1, sh1, lw["w_in"], tl, keep)

    if prompt:
        att = _attn_call(q, k_bf, k_bf, v_bf, v_bf, lw["bias"], lw["g_attn"], tq=tl, ra=tl, prompt=True)
    else:
        ck, cv = attend_kv
        att = _attn_call(q, ck, k_bf, cv, v_bf, lw["bias"], lw["g_attn"], tq=l, ra=ck.shape[1], prompt=False)

    ssm_tl = max(512 // b, 1)
    ssm_n, s_re, s_im = _ssm_call(u_tm.reshape(l * b, SSM_W), h0_re, h0_im, lw["ssm"], lw["w_glu"], lw["g_ssm"],
                                  nb=b, tl=min(ssm_tl, l))
    ssm_n = ssm_n.reshape(l, b * SSM_W)

    x1, h2, eidx, gates, rank, counts = _postmix_call(att, ssm_n, x, lw["w_out"], g1, sc2, sh2, lw["ln1_g"],
                                                      lw["ln1_b"], lw["wr_hi"], lw["wr_lo"], lw["e_bias"], tl)
    t = b * l
    n_blocks = -(-t * TOP_K // EXPERT_BLOCK) + N_EXPERTS
    starts, block_expert, n_used = _plan_call(counts.reshape(N_EXPERTS), n_blocks)
    tm = min(256, l)
    xs, dest = _scatter_call(starts, counts.reshape(N_EXPERTS), _tile_major(eidx, tm), _tile_major(rank, tm),
                             h2.reshape(t, d), n_blocks, tm)
    ys = _moe_call(block_expert, n_used, xs, lw["w_gate"], lw["w_up"], lw["w_down"])
    x2 = _final_call(dest, x1, h2, gates.transpose(0, 2, 1).reshape(t, TOP_K), g2, lw["sw_gate"], lw["sw_up"], lw["sw_down"], lw["ln2_g"],
                     lw["ln2_b"], ys, tm)
    return x2, k_keep, v_keep, s_re, s_im


def kernel(x_prompt, x_sample, cache_k, cache_v, state_ssm_re, state_ssm_im, c_prompt, c_sample, w_ada, b_ada, w_in, rel_bias, lam_re, lam_im, log_step, b_re, b_im, c_re, c_im, d_skip, w_glu, g_attn, g_ssm, w_out, ln1_g, ln1_b, w_router, e_bias, w_gate, w_up, w_down, sw_gate, sw_up, sw_down, ln2_g, ln2_b):
    bp, lp, d = x_prompt.shape
    bs, ls, _ = x_sample.shape
    depth = w_in.shape[0]

    mod_all = _ada_call(jnp.concatenate([c_prompt, c_sample], axis=0), w_ada, b_ada)

    def mods(l, lo, hi):
        m = mod_all[l, lo:hi].reshape(hi - lo, 6, 1, d)
        return tuple(m[:, i] for i in range(6))

    yp, ys = x_prompt, x_sample
    outs = [[] for _ in range(8)]
    zero_state = jnp.zeros((bp, SSM_CH), F32)
    for l in range(depth):
        wr = jnp.pad(w_router[l], ((0, 0), (0, LANES - N_EXPERTS)))
        wr_hi = wr.astype(BF16)
        wr_lo = (wr - wr_hi.astype(F32)).astype(BF16)
        lw = dict(
            w_in=w_in[l].astype(BF16), bias=_band_bias_table(rel_bias[l]), g_attn=g_attn[l].reshape(1, ATT_W),
            ssm=_ssm_params(lam_re[l], lam_im[l], log_step[l], b_re[l], b_im[l], c_re[l], c_im[l], d_skip[l]),
            w_glu=w_glu[l].astype(BF16), g_ssm=g_ssm[l].reshape(1, SSM_W), w_out=w_out[l].astype(BF16),
            ln1_g=ln1_g[l].reshape(1, d), ln1_b=ln1_b[l].reshape(1, d), wr_hi=wr_hi, wr_lo=wr_lo,
            e_bias=e_bias[l].reshape(N_EXPERTS, 1), w_gate=w_gate[l].astype(BF16), w_up=w_up[l].astype(BF16),
            w_down=w_down[l].astype(BF16), sw_gate=sw_gate[l].astype(BF16), sw_up=sw_up[l].astype(BF16),
            sw_down=sw_down[l].astype(BF16), ln2_g=ln2_g[l].reshape(1, d), ln2_b=ln2_b[l].reshape(1, d))

        yp, k, v, sr, si = _layer(yp, mods(l, 0, bp), None, zero_state, zero_state, lw, prompt=True)
        for o, a in zip(outs[0:4], (k.reshape(bp, -1, N_HEADS, HEAD_DIM), v.reshape(bp, -1, N_HEADS, HEAD_DIM),
                                    sr.reshape(bp, N_SSM_GROUPS, SSM_STATE), si.reshape(bp, N_SSM_GROUPS, SSM_STATE))):
            o.append(a)

        rows = cache_k.shape[2]
        ck = cache_k[l].reshape(bs, rows, ATT_W).astype(BF16)
        cv = cache_v[l].reshape(bs, rows, ATT_W).astype(BF16)
        ys, k, v, sr, si = _layer(ys, mods(l, bp, bp + bs), (ck, cv), state_ssm_re[l].reshape(bs, SSM_CH),
                                  state_ssm_im[l].reshape(bs, SSM_CH), lw, prompt=False)
        for o, a in zip(outs[4:8], (k.reshape(bs, -1, N_HEADS, HEAD_DIM), v.reshape(bs, -1, N_HEADS, HEAD_DIM),
                                    sr.reshape(bs, N_SSM_GROUPS, SSM_STATE), si.reshape(bs, N_SSM_GROUPS, SSM_STATE))):
            o.append(a)

    return (yp, ys) + tuple(jnp.stack(o) for o in outs)
```
